```python
import math
import jax
import jax.numpy as jnp
from jax import lax
import numpy as np

D_MODEL = 2048
BATCH = 2
SEQ = 4096
DEPTH = 2

HEAD_DIM = 128
LRU_BLOCKS = 6
LRU_BD = 128
LRU_W = LRU_BLOCKS * LRU_BD
LRU_C = 8.0
CONV_W = 4
SWA_HEADS = 6
SWA_W = SWA_HEADS * HEAD_DIM
DILATED_PATTERNS = ((128, 1), (512, 4), (2048, 16))
SB_HEADS = 4
SB_W = SB_HEADS * HEAD_DIM
MIX_W = LRU_W + SWA_W + SB_W
IN_W = 2 * LRU_W + 3 * SWA_W + 3 * SB_W
Q_BLOCK = 128
REL_BUCKETS = 32
REL_MAX_DIST = 2048
PEER_HEADS = 8
PEER_NKEYS = 128
PEER_EXPERTS = PEER_NKEYS * PEER_NKEYS
PEER_DQ = 256
PEER_TOPK = 16
TOK_BLOCK = 128
NEG_INF = -1e30
EPS = 1e-6

kernel_name = "hybrid_lru_dilated_stickbreak_peer_block"


def rms_norm(x, g):
    xf = x.astype(jnp.float32)
    y = xf * lax.rsqrt(jnp.mean(xf * xf, axis=-1, keepdims=True) + EPS)
    return (y * g.astype(jnp.float32)).astype(x.dtype)


def t5_bucket(dist):
    dist = np.asarray(dist, np.int32)
    max_exact = REL_BUCKETS // 2
    large = max_exact + (np.log(np.maximum(dist, 1) / max_exact)
                         / np.log(REL_MAX_DIST / max_exact)
                         * (REL_BUCKETS - max_exact)).astype(np.int32)
    large = np.minimum(large, REL_BUCKETS - 1)
    return np.where(dist < max_exact, dist, large).astype(np.int32)


def causal_depthwise_conv(x, w, b):
    C = x.shape[-1]
    y = lax.conv_general_dilated(x, w[:, None, :].astype(x.dtype), window_strides=(1,),
                                 padding=[(CONV_W - 1, 0)],
                                 dimension_numbers=('NWC', 'WIO', 'NWC'),
                                 feature_group_count=C)
    return y + b.astype(x.dtype)


def rg_lru(xa, w_rec, b_rec, w_in, b_in, lam):
    B, S, C = xa.shape
    xf = xa.astype(jnp.float32)
    xb = xf.reshape(B, S, LRU_BLOCKS, LRU_BD)
    r = jax.nn.sigmoid(jnp.einsum('bsnj,njk->bsnk', xb, w_rec.astype(jnp.float32)).reshape(B, S, C)
                       + b_rec.astype(jnp.float32))
    i = jax.nn.sigmoid(jnp.einsum('bsnj,njk->bsnk', xb, w_in.astype(jnp.float32)).reshape(B, S, C)
                       + b_in.astype(jnp.float32))
    log_a = -LRU_C * r * jax.nn.softplus(-lam.astype(jnp.float32))
    a = jnp.exp(log_a)
    u = jnp.sqrt(-jnp.expm1(2.0 * log_a)) * (i * xf)

    def combine(left, right):
        a_l, b_l = left
        a_r, b_r = right
        return a_l * a_r, a_r * b_l + b_r

    _, h = lax.associative_scan(combine, (a, u), axis=1)
    return h.astype(xa.dtype)


def dilated_attention(q, k, v, rel_bias):
    B, S, H, Dh = q.shape
    nb = S // Q_BLOCK
    scale = Dh ** -0.5
    pats = []
    for (w, d) in DILATED_PATTERNS:
        offs = d * np.arange(w // d + 1, dtype=np.int32)
        bias = rel_bias[t5_bucket(offs)].T.astype(jnp.float32)
        pats.append((jnp.asarray(offs), bias))

    def block(bi):
        t0 = bi * Q_BLOCK
        qb = lax.dynamic_slice_in_dim(q, t0, Q_BLOCK, axis=1)
        tpos = t0 + jnp.arange(Q_BLOCK)
        lses, outs = [], []
        for offs, bias in pats:
            pos = tpos[:, None] - offs[None, :]
            valid = pos >= 0
            posc = jnp.maximum(pos, 0)
            kg = jnp.take(k, posc, axis=1)
            vg = jnp.take(v, posc, axis=1)
            logits = jnp.einsum('bqhd,bqkhd->bhqk', qb, kg).astype(jnp.float32) * scale
            logits = logits + bias[None, :, None, :]
            logits = jnp.where(valid[None, None], logits, NEG_INF)
            m = jnp.max(logits, axis=-1, keepdims=True)
            p = jnp.exp(logits - m)
            s = jnp.sum(p, axis=-1, keepdims=True)
            o = jnp.einsum('bhqk,bqkhd->bqhd', (p / s).astype(vg.dtype), vg).astype(jnp.float32)
            lses.append((m + jnp.log(s))[..., 0])
            outs.append(o)
        wts = jax.nn.softmax(jnp.stack(lses, axis=0), axis=0)
        wts = jnp.transpose(wts, (0, 1, 3, 2))[..., None]
        return jnp.sum(wts * jnp.stack(outs, axis=0), axis=0).astype(q.dtype)

    out = lax.map(block, jnp.arange(nb))
    return jnp.transpose(out, (1, 0, 2, 3, 4)).reshape(B, S, H, Dh)


def stick_breaking_attention(q, k, v):
    B, S, H, Dh = q.shape
    nb = S // Q_BLOCK
    scale = Dh ** -0.5
    spos = jnp.arange(S)

    def block(bi):
        t0 = bi * Q_BLOCK
        qb = lax.dynamic_slice_in_dim(q, t0, Q_BLOCK, axis=1)
        z = jnp.einsum('bqhd,bshd->bhqs', qb, k).astype(jnp.float32) * scale
        tpos = t0 + jnp.arange(Q_BLOCK)
        causal = spos[None, :] < tpos[:, None]
        log1m = jnp.where(causal, jax.nn.log_sigmoid(-z), 0.0)
        after = lax.cumsum(log1m, axis=3, reverse=True) - log1m
        A = jnp.where(causal, jnp.exp(jax.nn.log_sigmoid(z) + after), 0.0)
        return jnp.einsum('bhqs,bshd->bqhd', A.astype(v.dtype), v)

    out = lax.map(block, jnp.arange(nb))
    return jnp.transpose(out, (1, 0, 2, 3, 4)).reshape(B, S, H, Dh)


def hybrid_mixer(h, w_in, conv_w, conv_b, w_rec, b_rec, w_gi, b_gi, lam, rel_bias,
                 g_lru, g_swa, g_sb, w_out):
    B, S, _ = h.shape
    z = h @ w_in
    cuts = np.cumsum([LRU_W, LRU_W, SWA_W, SWA_W, SWA_W, SB_W, SB_W]).tolist()
    xa, ga, qb, kb, vb, qc, kc, vc = jnp.split(z, cuts, axis=-1)
    xa = causal_depthwise_conv(xa, conv_w, conv_b)
    ya = rg_lru(xa, w_rec, b_rec, w_gi, b_gi, lam) * jax.nn.gelu(ga)
    hs = lambda t, n: t.reshape(B, S, n, HEAD_DIM)
    yb = dilated_attention(hs(qb, SWA_HEADS), hs(kb, SWA_HEADS), hs(vb, SWA_HEADS),
                           rel_bias).reshape(B, S, SWA_W)
    yc = stick_breaking_attention(hs(qc, SB_HEADS), hs(kc, SB_HEADS),
                                  hs(vc, SB_HEADS)).reshape(B, S, SB_W)
    y = jnp.concatenate([rms_norm(ya, g_lru), rms_norm(yb, g_swa), rms_norm(yc, g_sb)], axis=-1)
    return y @ w_out


def peer_ffn(h, wq, subkeys, u_tab, v_tab):
    B, S, D = h.shape
    T = B * S
    hf = h.reshape(T, D)
    q = (hf @ wq).reshape(T, PEER_HEADS, 2, PEER_DQ // 2)
    s = jnp.einsum('thcd,hckd->thck', q, subkeys).astype(jnp.float32)
    v_top, i_top = lax.top_k(s, PEER_TOPK)
    cand_s = (v_top[:, :, 0, :, None] + v_top[:, :, 1, None, :]).reshape(T, PEER_HEADS, -1)
    cand_i = (i_top[:, :, 0, :, None] * PEER_NKEYS + i_top[:, :, 1, None, :]).reshape(T, PEER_HEADS, -1)
    best_s, best_pos = lax.top_k(cand_s, PEER_TOPK)
    ids = jnp.take_along_axis(cand_i, best_pos, axis=-1)
    g = jax.nn.softmax(best_s, axis=-1)
    nblk = T // TOK_BLOCK
    HK = PEER_HEADS * PEER_TOPK

    def block(args):
        xb, idb, gb = args
        ub = jnp.take(u_tab, idb, axis=0)
        act = jax.nn.gelu(jnp.einsum('tkd,td->tk', ub, xb).astype(jnp.float32))
        vb = jnp.take(v_tab, idb, axis=0)
        return jnp.einsum('tk,tkd->td', (gb * act).astype(xb.dtype), vb)

    y = lax.map(block, (hf.reshape(nblk, TOK_BLOCK, D), ids.reshape(nblk, TOK_BLOCK, HK),
                        g.reshape(nblk, TOK_BLOCK, HK)))
    return y.reshape(B, S, D)


def setup_inputs(seed: int = 0) -> dict:
    key = jax.random.key(seed)
    ks = jax.random.split(key, 24)
    D = D_MODEL
    f32 = jnp.float32

    def nrm(k, shape, s):
        return jax.random.normal(k, shape, f32) * s

    u = jax.random.uniform(ks[13], (DEPTH, LRU_W), f32, minval=0.9, maxval=0.999)
    sig = u ** (1.0 / LRU_C)
    lam = jnp.log(sig) - jnp.log1p(-sig)
    return {
        "x": nrm(ks[0], (BATCH, SEQ, D), 1.0),
        "c": nrm(ks[1], (BATCH, D), 1.0),
        "w_mod": nrm(ks[2], (DEPTH, D, 6 * D), 0.5 * D ** -0.5),
        "b_mod": nrm(ks[3], (DEPTH, 6 * D), 0.02),
        "norm_mix": 1.0 + nrm(ks[4], (DEPTH, D), 0.02),
        "norm_ffn": 1.0 + nrm(ks[5], (DEPTH, D), 0.02),
        "w_in": nrm(ks[6], (DEPTH, D, IN_W), D ** -0.5),
        "conv_w": nrm(ks[7], (DEPTH, CONV_W, LRU_W), CONV_W ** -0.5),
        "conv_b": nrm(ks[8], (DEPTH, LRU_W), 0.02),
        "lru_w_rec": nrm(ks[9], (DEPTH, LRU_BLOCKS, LRU_BD, LRU_BD), LRU_BD ** -0.5),
        "lru_b_rec": nrm(ks[10], (DEPTH, LRU_W), 0.02),
        "lru_w_in": nrm(ks[11], (DEPTH, LRU_BLOCKS, LRU_BD, LRU_BD), LRU_BD ** -0.5),
        "lru_b_in": nrm(ks[12], (DEPTH, LRU_W), 0.02),
        "lru_lambda": lam,
        "rel_bias": nrm(ks[14], (REL_BUCKETS, SWA_HEADS), 0.1),
        "gnorm_lru": 1.0 + nrm(ks[15], (DEPTH, LRU_W), 0.02),
        "gnorm_swa": 1.0 + nrm(ks[16], (DEPTH, SWA_W), 0.02),
        "gnorm_sb": 1.0 + nrm(ks[17], (DEPTH, SB_W), 0.02),
        "w_out": nrm(ks[18], (DEPTH, MIX_W, D), MIX_W ** -0.5),
        "peer_wq": nrm(ks[19], (DEPTH, D, PEER_HEADS * PEER_DQ), D ** -0.5),
        "peer_subkeys": nrm(ks[20], (DEPTH, PEER_HEADS, 2, PEER_NKEYS, PEER_DQ // 2), (PEER_DQ // 2) ** -0.5),
        "peer_u": nrm(ks[21], (DEPTH, PEER_EXPERTS, D), D ** -0.5),
        "peer_v": nrm(ks[22], (DEPTH, PEER_EXPERTS, D), 1.0),
        "final_norm": 1.0 + nrm(ks[23], (D,), 0.02),
    }


def reference(x, c, w_mod, b_mod, norm_mix, norm_ffn, w_in, conv_w, conv_b,
              lru_w_rec, lru_b_rec, lru_w_in, lru_b_in, lru_lambda, rel_bias,
              gnorm_lru, gnorm_swa, gnorm_sb, w_out, peer_wq, peer_subkeys,
              peer_u, peer_v, final_norm):
    cs = jax.nn.silu(c)
    for l in range(DEPTH):
        mod = cs @ w_mod[l] + b_mod[l]
        sh1, sc1, g1, sh2, sc2, g2 = [m[:, None, :] for m in jnp.split(mod, 6, axis=-1)]
        h = rms_norm(x, norm_mix[l]) * (1.0 + sc1) + sh1
        x = x + g1 * hybrid_mixer(h, w_in[l], conv_w[l], conv_b[l], lru_w_rec[l], lru_b_rec[l],
                                  lru_w_in[l], lru_b_in[l], lru_lambda[l], rel_bias,
                                  gnorm_lru[l], gnorm_swa[l], gnorm_sb[l], w_out[l])
        h = rms_norm(x, norm_ffn[l]) * (1.0 + sc2) + sh2
        x = x + g2 * peer_ffn(h, peer_wq[l], peer_subkeys[l], peer_u[l], peer_v[l])
    return rms_norm(x, final_norm)
```

```python
import functools

import numpy as np
import jax
import jax.numpy as jnp
from jax import lax
from jax.experimental import pallas as pl
from jax.experimental.pallas import tpu as pltpu

F32 = jnp.float32
BF16 = jnp.bfloat16

HEAD_DIM = 128
LRU_BLOCKS = 6
LRU_W = LRU_BLOCKS * 128
LRU_C = 8.0
CONV_W = 4
SWA_HEADS = 6
SWA_W = SWA_HEADS * HEAD_DIM
DILATED_PATTERNS = ((128, 1), (512, 4), (2048, 16))
SB_HEADS = 4
SB_W = SB_HEADS * HEAD_DIM
REL_BUCKETS = 32
REL_MAX_DIST = 2048
PEER_HEADS = 8
PEER_NKEYS = 128
PEER_TOPK = 16
NEG_INF = -1e30
EPS = 1e-6

VMEM_LIMIT_BYTES = 56 * 1024 * 1024
TOKENS_PER_VREG = 8 * 128


def _cparams(*sem):
    return pltpu.CompilerParams(dimension_semantics=sem, vmem_limit_bytes=VMEM_LIMIT_BYTES)


def _rms(y, g):
    return y * lax.rsqrt(jnp.mean(y * y, axis=-1, keepdims=True) + EPS) * g


def _gelu(x):
    return 0.5 * x * (1.0 + jnp.tanh(0.7978845608028654 * (x + 0.044715 * (x * x * x))))


def _softplus(x):
    return jnp.maximum(x, 0.0) + jnp.log1p(jnp.exp(-jnp.abs(x)))


def _mod_kernel(c_ref, w_ref, b_ref, o_ref):
    c = c_ref[...]
    cs = c * jax.nn.sigmoid(c)
    o_ref[0] = jnp.dot(cs.astype(BF16), w_ref[0].astype(BF16),
                       preferred_element_type=F32) + b_ref[0]


def _modulation(c, w_mod, b_mod):
    depth, d, n = w_mod.shape
    bsz = c.shape[0]
    rows = 8
    c8 = jnp.pad(c, ((0, rows - bsz), (0, 0)))
    tn = 512
    out = pl.pallas_call(
        _mod_kernel,
        grid=(depth, n // tn),
        in_specs=[pl.BlockSpec((rows, d), lambda l, j: (0, 0)),
                  pl.BlockSpec((1, d, tn), lambda l, j: (l, 0, j)),
                  pl.BlockSpec((1, 1, tn), lambda l, j: (l, 0, j))],
        out_specs=pl.BlockSpec((1, rows, tn), lambda l, j: (l, 0, j)),
        out_shape=jax.ShapeDtypeStruct((depth, rows, n), F32),
        compiler_params=_cparams("parallel", "parallel"),
        name="mod",
    )(c8, w_mod, b_mod.reshape(depth, 1, n))
    return out[:, :bsz]


def _adaln(x, g, sc, sh):
    return _rms(x, g) * (1.0 + sc) + sh


def _inproj_kernel(x_ref, g_ref, sc_ref, sh_ref, w_ref, o_ref, h_s):
    @pl.when(pl.program_id(1) == 0)
    def _():
        h_s[...] = _adaln(x_ref[...], g_ref[...], sc_ref[0], sh_ref[0]).astype(BF16)

    o_ref[...] = jnp.dot(h_s[...], w_ref[...], preferred_element_type=F32).astype(o_ref.dtype)


def _inproj(x2, g, sc, sh, w, seq):
    t, d = x2.shape
    n = w.shape[1]
    tm = min(512, seq)
    tn = 768
    tpb = seq // tm
    return pl.pallas_call(
        _inproj_kernel,
        grid=(t // tm, n // tn),
        in_specs=[pl.BlockSpec((tm, d), lambda i, j: (i, 0)),
                  pl.BlockSpec((1, d), lambda i, j: (0, 0)),
                  pl.BlockSpec((1, 1, d), lambda i, j: (i // tpb, 0, 0)),
                  pl.BlockSpec((1, 1, d), lambda i, j: (i // tpb, 0, 0)),
                  pl.BlockSpec((d, tn), lambda i, j: (0, j))],
        out_specs=pl.BlockSpec((tm, tn), lambda i, j: (i, j)),
        out_shape=jax.ShapeDtypeStruct((t, n), BF16),
        scratch_shapes=[pltpu.VMEM((tm, d), BF16)],
        compiler_params=_cparams("parallel", "arbitrary"),
        name="inproj",
    )(x2, g, sc, sh, w)


def _lru_kernel(xa_ref, ga_ref, cw_ref, cb_ref, wr_ref, br_ref, wi_ref, bi_ref, lam_ref,
                o_ref, xbuf, a_s, u_s, h_s):
    tc = xa_ref.shape[0]
    width = xa_ref.shape[1]
    i = pl.program_id(1)

    @pl.when(i == 0)
    def _():
        xbuf[0:8, :] = jnp.zeros((8, width), F32)
        h_s[...] = jnp.zeros_like(h_s)

    @pl.when(i > 0)
    def _():
        xbuf[0:8, :] = xbuf[tc:tc + 8, :]

    xbuf[8:tc + 8, :] = xa_ref[...].astype(F32)
    cw = cw_ref[...]
    xc = cb_ref[...] + cw[CONV_W - 1:CONV_W, :] * xbuf[8:tc + 8, :]
    for k in range(CONV_W - 1):
        off = 8 - (CONV_W - 1) + k
        xc = xc + cw[k:k + 1, :] * xbuf[off:off + tc, :]

    neg_c_sp = -LRU_C * _softplus(-lam_ref[...])
    for n in range(width // 128):
        sl = slice(n * 128, (n + 1) * 128)
        xb = xc[:, sl]
        xb16 = xb.astype(BF16)
        r = jax.nn.sigmoid(jnp.dot(xb16, wr_ref[n], preferred_element_type=F32) + br_ref[:, sl])
        gi = jax.nn.sigmoid(jnp.dot(xb16, wi_ref[n], preferred_element_type=F32) + bi_ref[:, sl])
        log_a = neg_c_sp[:, sl] * r
        a = jnp.exp(log_a)
        a_s[:, sl] = a
        u_s[:, sl] = jnp.sqrt(-jnp.tanh(log_a) * (a * a + 1.0)) * (gi * xb)

    rows = lax.broadcasted_iota(jnp.int32, (8, width), 0)

    def body(g, h):
        r0 = pl.multiple_of(g * 8, 8)
        a8 = a_s[pl.ds(r0, 8), :]
        u8 = u_s[pl.ds(r0, 8), :]
        out = jnp.zeros((8, width), F32)
        for j in range(8):
            h = jnp.broadcast_to(a8[j:j + 1, :], (8, width)) * h \
                + jnp.broadcast_to(u8[j:j + 1, :], (8, width))
            out = jnp.where(rows == j, h, out)
        u_s[pl.ds(r0, 8), :] = out
        return h

    h_s[...] = lax.fori_loop(0, tc // 8, body, h_s[...])
    o_ref[...] = (u_s[...] * _gelu(ga_ref[...].astype(F32))).astype(o_ref.dtype)


def _lru(z, conv_w, conv_b, w_rec, b_rec, w_gi, b_gi, lam, bsz, seq):
    t = z.shape[0]
    tc = min(512, seq)
    nchunk = seq // tc
    w = LRU_W
    row = lambda i, col: (lambda b, i_: (b * nchunk + i_, col))
    full2 = lambda b, i: (0, 0)
    full3 = lambda b, i: (0, 0, 0)
    return pl.pallas_call(
        _lru_kernel,
        grid=(bsz, nchunk),
        in_specs=[pl.BlockSpec((tc, w), row(None, 0)),
                  pl.BlockSpec((tc, w), row(None, 1)),
                  pl.BlockSpec((CONV_W, w), full2),
                  pl.BlockSpec((1, w), full2),
                  pl.BlockSpec((LRU_BLOCKS, 128, 128), full3),
                  pl.BlockSpec((1, w), full2),
                  pl.BlockSpec((LRU_BLOCKS, 128, 128), full3),
                  pl.BlockSpec((1, w), full2),
                  pl.BlockSpec((1, w), full2)],
        out_specs=pl.BlockSpec((tc, w), row(None, 0)),
        out_shape=jax.ShapeDtypeStruct((t, w), BF16),
        scratch_shapes=[pltpu.VMEM((tc + 8, w), F32), pltpu.VMEM((tc, w), F32),
                        pltpu.VMEM((tc, w), F32), pltpu.VMEM((8, w), F32)],
        compiler_params=_cparams("parallel", "arbitrary"),
        name="lru",
    )(z, z, conv_w, conv_b.reshape(1, w), w_rec.astype(BF16), b_rec.reshape(1, w),
      w_gi.astype(BF16), b_gi.reshape(1, w), lam.reshape(1, w))


def _t5_bucket(dist):
    dist = np.asarray(dist, np.int32)
    max_exact = REL_BUCKETS // 2
    large = max_exact + (np.log(np.maximum(dist, 1) / max_exact)
                         / np.log(REL_MAX_DIST / max_exact)
                         * (REL_BUCKETS - max_exact)).astype(np.int32)
    large = np.minimum(large, REL_BUCKETS - 1)
    return np.where(dist < max_exact, dist, large).astype(np.int32)


def _dilated_kernel(q_ref, kc_ref, kp_ref, vc_ref, vp_ref, tz_ref, o_ref, l_ref, kbuf, vbuf):
    tq = q_ref.shape[1]
    mb = pl.program_id(2)
    kbuf[0:128, :] = kp_ref[0]
    kbuf[128:, :] = kc_ref[0]
    vbuf[0:128, :] = vp_ref[0]
    vbuf[128:, :] = vc_ref[0]
    scale = HEAD_DIM ** -0.5
    qi = lax.broadcasted_iota(jnp.int32, (128, 256), 0)
    ki = lax.broadcasted_iota(jnp.int32, (128, 256), 1)
    dist = qi + 128 - ki
    band = jnp.logical_and(dist >= 0, dist <= 128)
    band0 = jnp.logical_and(band, jnp.logical_or(ki >= 128, mb > 0))
    for s in range(tq // 128):
        mask = band0 if s == 0 else band
        rows = slice(s * 128, (s + 1) * 128)
        for h in range(SWA_HEADS):
            cols = slice(h * HEAD_DIM, (h + 1) * HEAD_DIM)
            q = q_ref[0, rows, cols]
            k = kbuf[s * 128:s * 128 + 256, cols]
            v = vbuf[s * 128:s * 128 + 256, cols]
            logits = lax.dot_general(q, k, (((1,), (1,)), ((), ())),
                                     preferred_element_type=F32) * scale + tz_ref[h]
            logits = jnp.where(mask, logits, NEG_INF)
            m = jnp.max(logits, axis=-1, keepdims=True)
            p = jnp.exp(logits - m)
            ssum = jnp.sum(p, axis=-1, keepdims=True)
            o = jnp.dot((p / ssum).astype(BF16), v, preferred_element_type=F32)
            o_ref[0, rows, cols] = o.astype(o_ref.dtype)
            l_ref[0, rows, cols] = jnp.broadcast_to(m + jnp.log(ssum), (128, HEAD_DIM))


def _dilated_pattern(src, col0, dil, tz, bsz, seq):
    sl = seq // dil
    tq = min(512, sl)
    nmb = sl // tq
    sub = tq // 128
    w = SWA_W
    qmap = lambda which: (lambda b, r, m: (b, m, col0 + 3 * r + which))
    pmap = lambda which: (lambda b, r, m: (b, jnp.maximum(m * sub - 1, 0), col0 + 3 * r + which))
    omap = lambda b, r, m: (b, m, r)
    return pl.pallas_call(
        _dilated_kernel,
        grid=(bsz, dil, nmb),
        in_specs=[pl.BlockSpec((1, tq, w), qmap(0)),
                  pl.BlockSpec((1, tq, w), qmap(1)),
                  pl.BlockSpec((1, 128, w), pmap(1)),
                  pl.BlockSpec((1, tq, w), qmap(2)),
                  pl.BlockSpec((1, 128, w), pmap(2)),
                  pl.BlockSpec((SWA_HEADS, 128, 256), lambda b, r, m: (0, 0, 0))],
        out_specs=[pl.BlockSpec((1, tq, w), omap), pl.BlockSpec((1, tq, w), omap)],
        out_shape=[jax.ShapeDtypeStruct((bsz, sl, dil * w), BF16),
                   jax.ShapeDtypeStruct((bsz, sl, dil * w), F32)],
        scratch_shapes=[pltpu.VMEM((tq + 128, w), BF16), pltpu.VMEM((tq + 128, w), BF16)],
        compiler_params=_cparams("parallel", "parallel", "arbitrary"),
        name=f"dilated{dil}",
    )(src, src, src, src, src, tz)


def _toeplitz_bias(rel_bias, dil):
    qi = np.arange(128)[:, None]
    ki = np.arange(256)[None, :]
    steps = np.clip(qi + 128 - ki, 0, 128)
    bucket = _t5_bucket(steps * dil)
    return jnp.transpose(rel_bias.astype(F32)[bucket], (2, 0, 1))


def _dilated(z3, rel_bias, bsz, seq):
    outs = []
    for (_, dil) in DILATED_PATTERNS:
        tz = _toeplitz_bias(rel_bias, dil)
        if dil == 1:
            src, col0 = z3, (2 * LRU_W) // SWA_W
        else:
            qkv = z3[:, :, 2 * LRU_W:2 * LRU_W + 3 * SWA_W]
            src, col0 = qkv.reshape(bsz, seq // dil, dil * 3 * SWA_W), 0
        o, l = _dilated_pattern(src, col0, dil, tz, bsz, seq)
        outs.append((o.reshape(bsz * seq, SWA_W), l.reshape(bsz * seq, SWA_W)))
    return outs


def _sb_kernel(q_ref, k_ref, v_ref, o_ref):
    tq = q_ref.shape[1]
    qi = pl.program_id(2)
    q = q_ref[0]
    scale = HEAD_DIM ** -0.5
    ri = lax.broadcasted_iota(jnp.int32, (tq, tq), 0)
    ci = lax.broadcasted_iota(jnp.int32, (tq, tq), 1)
    ones_ge = jnp.where(ri >= ci, 1.0, 0.0).astype(BF16)
    causal = ci < ri

    def tile(kb, carry, acc, diagonal):
        r0 = pl.multiple_of(kb * tq, tq)
        k = k_ref[0, pl.ds(r0, tq), :]
        v = v_ref[0, pl.ds(r0, tq), :]
        z = lax.dot_general(q, k, (((1,), (1,)), ((), ())), preferred_element_type=F32) * scale
        lsm = -_softplus(z)
        if diagonal:
            lsm = jnp.where(causal, lsm, 0.0)
        hi = lsm.astype(BF16)
        lo = (lsm - hi.astype(F32)).astype(BF16)
        incl = jnp.dot(hi, ones_ge, preferred_element_type=F32) \
            + jnp.dot(lo, ones_ge, preferred_element_type=F32)
        a = jnp.exp(z + incl + carry)
        if diagonal:
            a = jnp.where(causal, a, 0.0)
        acc = acc + jnp.dot(a.astype(BF16), v, preferred_element_type=F32)
        return carry + incl[:, 0:1], acc

    carry, acc = tile(qi, jnp.zeros((tq, 1), F32), jnp.zeros((tq, HEAD_DIM), F32), True)

    def body(it, ca):
        return tile(qi - 1 - it, ca[0], ca[1], False)

    carry, acc = lax.fori_loop(0, qi, body, (carry, acc))
    o_ref[0] = acc.astype(o_ref.dtype)


def _stickbreak(z3, bsz, seq):
    tq = min(256, seq)
    col0 = (2 * LRU_W + 3 * SWA_W) // HEAD_DIM
    return pl.pallas_call(
        _sb_kernel,
        grid=(bsz, SB_HEADS, seq // tq),
        in_specs=[pl.BlockSpec((1, tq, HEAD_DIM), lambda b, h, i: (b, i, col0 + h)),
                  pl.BlockSpec((1, seq, HEAD_DIM), lambda b, h, i: (b, 0, col0 + SB_HEADS + h)),
                  pl.BlockSpec((1, seq, HEAD_DIM), lambda b, h, i: (b, 0, col0 + 2 * SB_HEADS + h))],
        out_specs=pl.BlockSpec((1, tq, HEAD_DIM), lambda b, h, i: (b, i, h)),
        out_shape=jax.ShapeDtypeStruct((bsz, seq, SB_W), BF16),
        compiler_params=_cparams("parallel", "parallel", "arbitrary"),
        name="stickbreak",
    )(z3, z3, z3)


def _outproj_kernel(x_ref, ya_ref, o1_ref, o2_ref, o3_ref, l1_ref, l2_ref, l3_ref, yc_ref,
                    ga_ref, gb_ref, gc_ref, w_ref, g1_ref, out_ref):
    ya = _rms(ya_ref[...].astype(F32), ga_ref[...])
    l1, l2, l3 = l1_ref[...], l2_ref[...], l3_ref[...]
    m = jnp.maximum(jnp.maximum(l1, l2), l3)
    e1, e2, e3 = jnp.exp(l1 - m), jnp.exp(l2 - m), jnp.exp(l3 - m)
    yb = (e1 * o1_ref[...].astype(F32) + e2 * o2_ref[...].astype(F32)
          + e3 * o3_ref[...].astype(F32)) / (e1 + e2 + e3)
    yb = _rms(yb, gb_ref[...])
    yc = _rms(yc_ref[...].astype(F32), gc_ref[...])
    y = jnp.dot(ya.astype(BF16), w_ref[0:LRU_W, :], preferred_element_type=F32)
    y = y + jnp.dot(yb.astype(BF16), w_ref[LRU_W:LRU_W + SWA_W, :], preferred_element_type=F32)
    y = y + jnp.dot(yc.astype(BF16), w_ref[LRU_W + SWA_W:, :], preferred_element_type=F32)
    out_ref[...] = x_ref[...] + g1_ref[0] * y


def _outproj(x2, ya, dil_outs, yc, g_lru, g_swa, g_sb, w_out, g1, seq):
    t, d = x2.shape
    tm = min(512, seq)
    tpb = seq // tm
    row = lambda width: pl.BlockSpec((tm, width), lambda i: (i, 0))
    vec = lambda width: pl.BlockSpec((1, width), lambda i: (0, 0))
    (o1, l1), (o2, l2), (o3, l3) = dil_outs
    return pl.pallas_call(
        _outproj_kernel,
        grid=(t // tm,),
        in_specs=[row(d), row(LRU_W), row(SWA_W), row(SWA_W), row(SWA_W),
                  row(SWA_W), row(SWA_W), row(SWA_W), row(SB_W),
                  vec(LRU_W), vec(SWA_W), vec(SB_W),
                  pl.BlockSpec((d, d), lambda i: (0, 0)),
                  pl.BlockSpec((1, 1, d), lambda i: (i // tpb, 0, 0))],
        out_specs=row(d),
        out_shape=jax.ShapeDtypeStruct((t, d), F32),
        compiler_params=_cparams("parallel"),
        name="outproj",
    )(x2, ya, o1, o2, o3, l1, l2, l3, yc, g_lru.reshape(1, -1), g_swa.reshape(1, -1),
      g_sb.reshape(1, -1), w_out, g1)


def _peerq_kernel(x_ref, g_ref, sc_ref, sh_ref, wq_ref, sk_ref, h_ref, s_ref):
    h = _adaln(x_ref[...], g_ref[...], sc_ref[0], sh_ref[0]).astype(BF16)
    h_ref[...] = h
    q = jnp.dot(h, wq_ref[...], preferred_element_type=F32).astype(BF16)
    for hc in range(2 * PEER_HEADS):
        qs = q[:, hc * 128:(hc + 1) * 128]
        s_ref[hc] = lax.dot_general(sk_ref[hc], qs, (((1,), (1,)), ((), ())),
                                    preferred_element_type=F32)


def _peer_q(x2, g, sc, sh, wq, subkeys, seq):
    t, d = x2.shape
    tm = min(512, seq)
    tpb = seq // tm
    nhc = 2 * PEER_HEADS
    return pl.pallas_call(
        _peerq_kernel,
        grid=(t // tm,),
        in_specs=[pl.BlockSpec((tm, d), lambda i: (i, 0)),
                  pl.BlockSpec((1, d), lambda i: (0, 0)),
                  pl.BlockSpec((1, 1, d), lambda i: (i // tpb, 0, 0)),
                  pl.BlockSpec((1, 1, d), lambda i: (i // tpb, 0, 0)),
                  pl.BlockSpec((d, nhc * 128), lambda i: (0, 0)),
                  pl.BlockSpec((nhc, PEER_NKEYS, 128), lambda i: (0, 0, 0))],
        out_specs=[pl.BlockSpec((tm, d), lambda i: (i, 0)),
                   pl.BlockSpec((nhc, PEER_NKEYS, tm), lambda i: (0, 0, i))],
        out_shape=[jax.ShapeDtypeStruct((t, d), BF16),
                   jax.ShapeDtypeStruct((nhc, PEER_NKEYS, t), F32)],
        compiler_params=_cparams("parallel"),
        name="peer_q",
    )(x2, g, sc, sh, wq, subkeys)


_NRANK = PEER_TOPK + 1
_PAIR_RANKS = [(p, q) for p in range(_NRANK) for q in range(_NRANK) if (p + 1) * (q + 1) <= _NRANK]


def _topk_kernel(s_ref, tau_ref, a1_ref, b1_ref, z_ref, work):
    def top_ranks(c):
        work[...] = s_ref[c, :, 0]
        tops = []
        for p in range(_NRANK):
            m = jnp.max(work[...], axis=0)
            tops.append(m)
            if p + 1 < _NRANK:
                work[...] = jnp.where(work[...] == m[None], -jnp.inf, work[...])
        return tops

    a = top_ranks(0)
    b = top_ranks(1)
    cands = [a[p] + b[q] for (p, q) in _PAIR_RANKS]
    cur = list(cands)
    ranked = []
    for it in range(_NRANK):
        m = functools.reduce(jnp.maximum, cur)
        ranked.append(m)
        if it + 1 < _NRANK:
            cur = [jnp.where(c == m, -jnp.inf, c) for c in cur]
    tau = 0.5 * (ranked[PEER_TOPK - 1] + ranked[PEER_TOPK])
    top = a[0] + b[0]
    zsum = functools.reduce(
        jnp.add, [jnp.where(c >= tau, jnp.exp(c - top), 0.0) for c in cands])
    tau_ref[0, 0] = tau
    a1_ref[0, 0] = a[0]
    b1_ref[0, 0] = b[0]
    z_ref[0, 0] = zsum


def _peer_topk(s_t):
    nhc, nk, t = s_t.shape
    nb = t // TOKENS_PER_VREG
    s5 = s_t.reshape(nhc, nk, nb, 8, 128)
    stat = jax.ShapeDtypeStruct((PEER_HEADS, nb, 8, 128), F32)
    ospec = pl.BlockSpec((1, 1, 8, 128), lambda tb, h: (h, tb, 0, 0))
    outs = pl.pallas_call(
        _topk_kernel,
        grid=(nb, PEER_HEADS),
        in_specs=[pl.BlockSpec((2, nk, 1, 8, 128), lambda tb, h: (h, 0, tb, 0, 0))],
        out_specs=[ospec] * 4,
        out_shape=[stat] * 4,
        scratch_shapes=[pltpu.VMEM((nk, 8, 128), F32)],
        compiler_params=_cparams("parallel", "parallel"),
        name="peer_topk",
    )(s5)
    return [o.reshape(PEER_HEADS, t) for o in outs]


def _peer_dense_kernel(h_ref, s1_ref, s2_ref, tau_ref, a1_ref, b1_ref, z_ref, u_ref, vt_ref,
                       x_ref, g2_ref, gf_ref, out_ref, theta_s, e1_s, e2_s, p_s, acc_s,
                       *, final_norm):
    e = pl.program_id(1)
    et = u_ref.shape[0]
    tt = h_ref.shape[0]
    ni = et // PEER_NKEYS

    @pl.when(e == 0)
    def _():
        for h in range(PEER_HEADS):
            hs = slice(h, h + 1)
            s1 = s1_ref[h]
            theta_s[h] = tau_ref[hs, :] - s1
            e1_s[h] = jnp.exp(s1 - a1_ref[hs, :])
            e2_s[h] = jnp.exp(s2_ref[h] - b1_ref[hs, :]) / z_ref[hs, :]
        acc_s[...] = jnp.zeros_like(acc_s)

    act = lax.dot_general(u_ref[...], h_ref[...], (((1,), (1,)), ((), ())),
                          preferred_element_type=F32)
    for il in range(ni):
        i = e * ni + il
        w = jnp.zeros((PEER_NKEYS, tt), F32)
        for h in range(PEER_HEADS):
            th = theta_s[h, pl.ds(i, 1), :]
            e1 = e1_s[h, pl.ds(i, 1), :]
            w = w + jnp.where(s2_ref[h] >= th, e2_s[h] * e1, 0.0)
        rows = slice(il * PEER_NKEYS, (il + 1) * PEER_NKEYS)
        p_s[rows, :] = (w * _gelu(act[rows, :])).astype(BF16)
    acc_s[...] += jnp.dot(vt_ref[...], p_s[...], preferred_element_type=F32)

    @pl.when(e == pl.num_programs(1) - 1)
    def _():
        y = x_ref[...] + g2_ref[0] * acc_s[...].T
        if final_norm:
            y = _rms(y, gf_ref[...])
        out_ref[...] = y


def _peer_dense(h2, s_t, stats, u, vt, x2, g2, gfinal, seq, final_norm):
    t, d = x2.shape
    nexp = u.shape[0]
    tt = min(512, seq)
    et = 512
    tpb = seq // tt
    s4 = s_t.reshape(PEER_HEADS, 2, PEER_NKEYS, t)
    stat_spec = pl.BlockSpec((PEER_HEADS, tt), lambda i, e: (0, i))
    sspec = lambda c: pl.BlockSpec((PEER_HEADS, None, PEER_NKEYS, tt), lambda i, e: (0, c, 0, i))
    big = (PEER_HEADS, PEER_NKEYS, tt)
    return pl.pallas_call(
        functools.partial(_peer_dense_kernel, final_norm=final_norm),
        grid=(t // tt, nexp // et),
        in_specs=[pl.BlockSpec((tt, d), lambda i, e: (i, 0)),
                  sspec(0), sspec(1),
                  stat_spec, stat_spec, stat_spec, stat_spec,
                  pl.BlockSpec((et, d), lambda i, e: (e, 0)),
                  pl.BlockSpec((d, et), lambda i, e: (0, e)),
                  pl.BlockSpec((tt, d), lambda i, e: (i, 0)),
                  pl.BlockSpec((1, 1, d), lambda i, e: (i // tpb, 0, 0)),
                  pl.BlockSpec((1, d), lambda i, e: (0, 0))],
        out_specs=pl.BlockSpec((tt, d), lambda i, e: (i, 0)),
        out_shape=jax.ShapeDtypeStruct((t, d), F32),
        scratch_shapes=[pltpu.VMEM(big, F32), pltpu.VMEM(big, F32), pltpu.VMEM(big, F32),
                        pltpu.VMEM((et, tt), BF16), pltpu.VMEM((d, tt), F32)],
        compiler_params=_cparams("parallel", "arbitrary"),
        name="peer_dense",
    )(h2, s4, s4, *stats, u, vt, x2, g2, gfinal)


def kernel(x, c, w_mod, b_mod, norm_mix, norm_ffn, w_in, conv_w, conv_b, lru_w_rec, lru_b_rec,
           lru_w_in, lru_b_in, lru_lambda, rel_bias, gnorm_lru, gnorm_swa, gnorm_sb, w_out,
           peer_wq, peer_subkeys, peer_u, peer_v, final_norm):
    bsz, seq, d = x.shape
    depth = w_mod.shape[0]
    t = bsz * seq
    mod = _modulation(c, w_mod, b_mod)
    x2 = x.reshape(t, d)
    gfinal = final_norm.reshape(1, d)
    for l in range(depth):
        sh1, sc1, g1, sh2, sc2, g2 = [mod[l, :, i * d:(i + 1) * d].reshape(bsz, 1, d)
                                      for i in range(6)]
        z = _inproj(x2, norm_mix[l].reshape(1, d), sc1, sh1, w_in[l].astype(BF16), seq)
        z3 = z.reshape(bsz, seq, -1)
        ya = _lru(z, conv_w[l], conv_b[l], lru_w_rec[l], lru_b_rec[l], lru_w_in[l], lru_b_in[l],
                  lru_lambda[l], bsz, seq)
        dil = _dilated(z3, rel_bias, bsz, seq)
        yc = _stickbreak(z3, bsz, seq).reshape(t, SB_W)
        x2 = _outproj(x2, ya, dil, yc, gnorm_lru[l], gnorm_swa[l], gnorm_sb[l],
                      w_out[l].astype(BF16), g1, seq)
        h2, s_t = _peer_q(x2, norm_ffn[l].reshape(1, d), sc2, sh2, peer_wq[l].astype(BF16),
                          peer_subkeys[l].reshape(2 * PEER_HEADS, PEER_NKEYS, -1).astype(BF16), seq)
        stats = _peer_topk(s_t)
        x2 = _peer_dense(h2, s_t, stats, peer_u[l].astype(BF16), peer_v[l].T.astype(BF16),
                         x2, g2, gfinal, seq, final_norm=(l == depth - 1))
    return x2.reshape(bsz, seq, d)
```
